```python
import jax, jax.numpy as jnp
from jax import lax
import numpy as np

D_MODEL = 1024
BATCH = 16
SEQ = 256
DEPTH = 2
DEC_BATCH = 2
DEC_SEQ = 1024
PAST_LEN = 512

GRID_W = 64
EPS = 1e-6
ROPE_THETA = 10000.0
BLOCK = 128
N_BRANCH = 3
A_HEADS = 8
A_NOPE = 64
A_ROPE = 32
A_QK = A_NOPE + A_ROPE
A_V = 64
Q_LORA = 384
KV_LORA = 256
B_GROUPS = 8
B_WIDTH = 512
B_CHUNK = 128
C_HEADS = 8
C_KV_HEADS = 2
C_GROUP = C_HEADS // C_KV_HEADS
C_HEAD_DIM = 64
WINDOW = 128
P_HEADS = 8
N_KEYS = 128
N_EXPERTS = N_KEYS * N_KEYS
P_QDIM = 256
P_TOPK = 16
TOKEN_BLOCK = 128
IN_SIZES = (Q_LORA, KV_LORA, A_ROPE, 2 * B_WIDTH, C_HEADS * C_HEAD_DIM,
            C_KV_HEADS * C_HEAD_DIM, C_KV_HEADS * C_HEAD_DIM, N_BRANCH * D_MODEL)
IN_DIM = sum(IN_SIZES)
IN_OFFSETS = tuple(sum(IN_SIZES[:i + 1]) for i in range(len(IN_SIZES) - 1))

kernel_name = 'hybrid_diffusion_prefix_trunk_step'


def rmsnorm(x, g):
    xf = x.astype(jnp.float32)
    y = xf * lax.rsqrt(jnp.mean(xf * xf, axis=-1, keepdims=True) + EPS)
    return (y * g.astype(jnp.float32)).astype(x.dtype)


def axial_rope(n_tokens, dim):
    rows = n_tokens // GRID_W
    row = jnp.repeat(jnp.arange(rows, dtype=jnp.float32), GRID_W)
    col = jnp.tile(jnp.arange(GRID_W, dtype=jnp.float32), rows)
    quarter = dim // 4
    inv_freq = ROPE_THETA ** (-jnp.arange(quarter, dtype=jnp.float32) / quarter)
    ang = jnp.concatenate([row[:, None] * inv_freq, col[:, None] * inv_freq], axis=-1)
    return jnp.cos(ang), jnp.sin(ang)


def apply_rope(x, cos, sin):
    half = x.shape[-1] // 2
    shape = (x.shape[1],) + (1,) * (x.ndim - 3) + (half,)
    c, s = cos.reshape(shape), sin.reshape(shape)
    x1, x2 = x[..., :half], x[..., half:]
    return jnp.concatenate([x1 * c - x2 * s, x2 * c + x1 * s], axis=-1).astype(x.dtype)


def softmax_with_sink(s, sink):
    m = jnp.maximum(jnp.max(s, axis=-1, keepdims=True), sink)
    e = jnp.exp(s - m)
    return e / (jnp.sum(e, axis=-1, keepdims=True) + jnp.exp(sink - m))


def dense_attention(q, k, v, sink=None):
    b, sq, hk, g, d = q.shape
    scale = d ** -0.5
    qb = jnp.moveaxis(q.reshape(b, sq // BLOCK, BLOCK, hk, g, d), 1, 0)

    def attend(q_blk):
        s = jnp.einsum('bqhgd,bkhd->bhgqk', q_blk, k).astype(jnp.float32) * scale
        if sink is None:
            p = jax.nn.softmax(s, axis=-1)
        else:
            p = softmax_with_sink(s, sink.astype(jnp.float32)[None, :, :, None, None])
        return jnp.einsum('bhgqk,bkhd->bqhgd', p.astype(v.dtype), v)

    o = lax.map(attend, qb)
    return jnp.moveaxis(o, 0, 1).reshape(b, sq, hk, g, v.shape[-1])


def windowed_attention(q, k, v, k_ctx, v_ctx, sink):
    b, s, hk, g, d = q.shape
    nb = s // BLOCK
    scale = d ** -0.5
    qb = q.reshape(b, nb, BLOCK, hk, g, d)

    def bands(t):
        tp = jnp.pad(t, ((0, 0), (BLOCK, BLOCK), (0, 0), (0, 0)))
        blk = tp.reshape(b, nb + 2, BLOCK, hk, t.shape[-1])
        return jnp.concatenate([blk[:, :-2], blk[:, 1:-1], blk[:, 2:]], axis=2)

    kb, vb = bands(k), bands(v)
    s_loc = jnp.einsum('bnqhgd,bnkhd->bnhgqk', qb, kb).astype(jnp.float32) * scale
    qpos = jnp.arange(nb)[:, None] * BLOCK + jnp.arange(BLOCK)[None, :]
    kpos = jnp.arange(nb)[:, None] * BLOCK - BLOCK + jnp.arange(3 * BLOCK)[None, :]
    valid = ((jnp.abs(qpos[:, :, None] - kpos[:, None, :]) <= WINDOW)
             & (kpos >= 0)[:, None, :] & (kpos < s)[:, None, :])
    s_loc = jnp.where(valid[None, :, None, None], s_loc, jnp.finfo(jnp.float32).min)
    s_ctx = jnp.einsum('bnqhgd,blhd->bnhgql', qb, k_ctx).astype(jnp.float32) * scale
    p = softmax_with_sink(jnp.concatenate([s_loc, s_ctx], axis=-1),
                          sink.astype(jnp.float32)[None, None, :, :, None, None])
    p = p.astype(v.dtype)
    o = (jnp.einsum('bnhgqk,bnkhd->bnqhgd', p[..., :3 * BLOCK], vb)
         + jnp.einsum('bnhgql,blhd->bnqhgd', p[..., 3 * BLOCK:], v_ctx))
    return o.reshape(b, s, hk, g, d)


def mla_keys_values(ckv_n, krope, w_ukv, g_k):
    b, s, _ = ckv_n.shape
    kv = (ckv_n @ w_ukv).reshape(b, s, A_HEADS, A_NOPE + A_V)
    k_nope, v = kv[..., :A_NOPE], kv[..., A_NOPE:]
    k_rope = jnp.broadcast_to(krope[:, :, None, :], (b, s, A_HEADS, A_ROPE))
    return rmsnorm(jnp.concatenate([k_nope, k_rope], axis=-1), g_k), v


def rope_tail(x, cos, sin):
    return jnp.concatenate([x[..., :A_NOPE], apply_rope(x[..., A_NOPE:], cos, sin)], axis=-1)


def chunk_mlp(z, g_v, w_s, b_s):
    z = jax.nn.gelu(z)
    u, v = z[..., :B_WIDTH], z[..., B_WIDTH:]
    v = rmsnorm(v, g_v)
    b, s, _ = v.shape
    vc = v.reshape(b, s // B_CHUNK, B_CHUNK, B_GROUPS, B_WIDTH // B_GROUPS)
    vm = jnp.einsum('gpq,bnqgc->bnpgc', w_s, vc) + b_s.T[None, None, :, :, None]
    return u * vm.reshape(b, s, B_WIDTH)


def peer(h, w_pq, keys, u_tab, v_tab):
    b, s, d = h.shape
    hb_all = h.reshape(-1, TOKEN_BLOCK, d)

    def one(hb):
        t = hb.shape[0]
        q = (hb @ w_pq).reshape(t, P_HEADS, 2, P_QDIM // 2)
        sc = jnp.einsum('thpd,hpkd->thpk', q, keys).astype(jnp.float32)
        sv, si = lax.top_k(sc, P_TOPK)
        cand = (sv[:, :, 0, :, None] + sv[:, :, 1, None, :]).reshape(t, P_HEADS, P_TOPK * P_TOPK)
        cidx = (si[:, :, 0, :, None] * N_KEYS + si[:, :, 1, None, :]).reshape(t, P_HEADS, P_TOPK * P_TOPK)
        cv, ci = lax.top_k(cand, P_TOPK)
        eidx = jnp.take_along_axis(cidx, ci, axis=-1)
        gate = jax.nn.softmax(cv, axis=-1)
        act = jax.nn.gelu(jnp.einsum('thkd,td->thk', u_tab[eidx], hb).astype(jnp.float32))
        return jnp.einsum('thk,thkd->td', (gate * act).astype(hb.dtype), v_tab[eidx])

    return lax.map(one, hb_all).reshape(b, s, d)


def branches(h, lw, rope, ctx_cache):
    b, s, _ = h.shape
    cq, ckv, krope, zb, qc, kc, vc, gate_logits = jnp.split(h @ lw['w_in'], IN_OFFSETS, axis=-1)
    ckv = rmsnorm(ckv, lw['g_kv_lora'])
    qa = rmsnorm((rmsnorm(cq, lw['g_q_lora']) @ lw['w_uq']).reshape(b, s, A_HEADS, A_QK), lw['g_qk_a_q'])
    ka, va = mla_keys_values(ckv, krope, lw['w_ukv'], lw['g_qk_a_k'])
    qc = rmsnorm(qc.reshape(b, s, C_KV_HEADS, C_GROUP, C_HEAD_DIM), lw['g_qk_c_q'])
    kc = rmsnorm(kc.reshape(b, s, C_KV_HEADS, C_HEAD_DIM), lw['g_qk_c_k'])
    vc = vc.reshape(b, s, C_KV_HEADS, C_HEAD_DIM)
    sink = lw['sink_c'].reshape(C_KV_HEADS, C_GROUP)
    if ctx_cache is None:
        oa = dense_attention(qa[:, :, :, None], ka, va)
        oc = dense_attention(qc, kc, vc, sink)
        new_ctx = (ckv, krope, kc, vc)
    else:
        cos_a, sin_a, cos_c, sin_c = rope
        ctx_ckv, ctx_krope, ctx_k, ctx_v = ctx_cache
        ka_ctx, va_ctx = mla_keys_values(ctx_ckv, ctx_krope, lw['w_ukv'], lw['g_qk_a_k'])
        qa = rope_tail(qa, cos_a, sin_a)
        ka = rope_tail(ka, cos_a, sin_a)
        oa = dense_attention(qa[:, :, :, None], jnp.concatenate([ka, ka_ctx], axis=1),
                             jnp.concatenate([va, va_ctx], axis=1))
        oc = windowed_attention(apply_rope(qc, cos_c, sin_c), apply_rope(kc, cos_c, sin_c), vc,
                                ctx_k, ctx_v, sink)
        new_ctx = None
    ob = chunk_mlp(zb, lw['g_v_b'], lw['w_s_b'], lw['b_s_b'])
    gates = jax.nn.sigmoid(gate_logits.reshape(b, s, N_BRANCH, D_MODEL))
    merged = (gates[:, :, 0] * (oa.reshape(b, s, A_HEADS * A_V) @ lw['w_o_a'])
              + gates[:, :, 1] * (ob @ lw['w_o_b'])
              + gates[:, :, 2] * (oc.reshape(b, s, C_HEADS * C_HEAD_DIM) @ lw['w_o_c']))
    return merged @ lw['w_out'], new_ctx


def layer(x, cond, lw, rope, ctx_cache):
    sh1, sc1, g1, sh2, sc2, g2 = jnp.split(jax.nn.silu(cond) @ lw['w_mod'] + lw['b_mod'], 6, axis=-1)
    h = rmsnorm(x, lw['g_attn_norm']) * (1 + sc1) + sh1
    mix, new_ctx = branches(h, lw, rope, ctx_cache)
    x = x + g1 * mix
    h = rmsnorm(x, lw['g_ffn_norm']) * (1 + sc2) + sh2
    x = x + g2 * peer(h, lw['w_pq'], lw['peer_keys'], lw['peer_u'], lw['peer_v'])
    return x, new_ctx


def setup_inputs(seed: int = 0) -> dict:
    key = jax.random.key(seed)
    keys = jax.random.split(key, 40)
    counter = iter(range(40))
    f32 = jnp.float32
    L, D = DEPTH, D_MODEL

    def nrm(shape, scale=1.0):
        return jax.random.normal(keys[next(counter)], shape, f32) * scale

    def gain(shape):
        return 1.0 + 0.02 * jax.random.normal(keys[next(counter)], shape, f32)

    return {
        'x_prompt': nrm((BATCH, SEQ, D)),
        'x_sample': nrm((DEC_BATCH, DEC_SEQ, D)),
        'c': nrm((DEC_BATCH, D)),
        'cache_mla_ckv': nrm((DEC_BATCH, L, PAST_LEN, KV_LORA)),
        'cache_mla_krope': nrm((DEC_BATCH, L, PAST_LEN, A_ROPE)),
        'cache_swa_k': nrm((DEC_BATCH, L, PAST_LEN, C_KV_HEADS, C_HEAD_DIM)),
        'cache_swa_v': nrm((DEC_BATCH, L, PAST_LEN, C_KV_HEADS, C_HEAD_DIM)),
        'c_ctx': nrm((D,)),
        'w_mod': nrm((L, D, 6 * D), D ** -0.5),
        'b_mod': nrm((L, 6 * D), 0.02),
        'g_attn_norm': gain((L, D)),
        'w_in': nrm((L, D, IN_DIM), D ** -0.5),
        'g_q_lora': gain((L, Q_LORA)),
        'w_uq': nrm((L, Q_LORA, A_HEADS * A_QK), Q_LORA ** -0.5),
        'g_kv_lora': gain((L, KV_LORA)),
        'w_ukv': nrm((L, KV_LORA, A_HEADS * (A_NOPE + A_V)), KV_LORA ** -0.5),
        'g_qk_a_q': gain((L, A_QK)),
        'g_qk_a_k': gain((L, A_QK)),
        'w_o_a': nrm((L, A_HEADS * A_V, D), (A_HEADS * A_V) ** -0.5),
        'g_v_b': gain((L, B_WIDTH)),
        'w_s_b': nrm((L, B_GROUPS, B_CHUNK, B_CHUNK), B_CHUNK ** -0.5),
        'b_s_b': gain((L, B_GROUPS, B_CHUNK)),
        'w_o_b': nrm((L, B_WIDTH, D), B_WIDTH ** -0.5),
        'g_qk_c_q': gain((L, C_HEAD_DIM)),
        'g_qk_c_k': gain((L, C_HEAD_DIM)),
        'sink_c': nrm((L, C_HEADS), 0.5),
        'w_o_c': nrm((L, C_HEADS * C_HEAD_DIM, D), (C_HEADS * C_HEAD_DIM) ** -0.5),
        'w_out': nrm((L, D, D), D ** -0.5),
        'g_ffn_norm': gain((L, D)),
        'w_pq': nrm((L, D, P_HEADS * P_QDIM), D ** -0.5),
        'peer_keys': nrm((L, P_HEADS, 2, N_KEYS, P_QDIM // 2), (P_QDIM // 2) ** -0.5),
        'peer_u': nrm((L, N_EXPERTS, D), D ** -0.5),
        'peer_v': nrm((L, N_EXPERTS, D), P_HEADS ** -0.5),
    }


def reference(x_prompt, x_sample, c, cache_mla_ckv, cache_mla_krope, cache_swa_k, cache_swa_v, c_ctx,
              w_mod, b_mod, g_attn_norm, w_in, g_q_lora, w_uq, g_kv_lora, w_ukv, g_qk_a_q, g_qk_a_k,
              w_o_a, g_v_b, w_s_b, b_s_b, w_o_b, g_qk_c_q, g_qk_c_k, sink_c, w_o_c, w_out, g_ffn_norm,
              w_pq, peer_keys, peer_u, peer_v):
    n_lat = x_sample.shape[1]
    rope = axial_rope(n_lat, A_ROPE) + axial_rope(n_lat, C_HEAD_DIM)
    ctx_cond = c_ctx[None, None, :]
    lat_cond = c[:, None, :]
    y_prompt, y_sample = x_prompt, x_sample
    ckv_list, krope_list, k_list, v_list = [], [], [], []
    for l in range(DEPTH):
        lw = {
            'w_mod': w_mod[l], 'b_mod': b_mod[l], 'g_attn_norm': g_attn_norm[l], 'w_in': w_in[l],
            'g_q_lora': g_q_lora[l], 'w_uq': w_uq[l], 'g_kv_lora': g_kv_lora[l], 'w_ukv': w_ukv[l],
            'g_qk_a_q': g_qk_a_q[l], 'g_qk_a_k': g_qk_a_k[l], 'w_o_a': w_o_a[l], 'g_v_b': g_v_b[l],
            'w_s_b': w_s_b[l], 'b_s_b': b_s_b[l], 'w_o_b': w_o_b[l], 'g_qk_c_q': g_qk_c_q[l],
            'g_qk_c_k': g_qk_c_k[l], 'sink_c': sink_c[l], 'w_o_c': w_o_c[l], 'w_out': w_out[l],
            'g_ffn_norm': g_ffn_norm[l], 'w_pq': w_pq[l], 'peer_keys': peer_keys[l],
            'peer_u': peer_u[l], 'peer_v': peer_v[l],
        }
        y_prompt, (ckv_l, krope_l, k_l, v_l) = layer(y_prompt, ctx_cond, lw, None, None)
        ckv_list.append(ckv_l)
        krope_list.append(krope_l)
        k_list.append(k_l)
        v_list.append(v_l)
        ctx_cache = (cache_mla_ckv[:, l], cache_mla_krope[:, l], cache_swa_k[:, l], cache_swa_v[:, l])
        y_sample, _ = layer(y_sample, lat_cond, lw, rope, ctx_cache)
    new_mla_ckv = jnp.stack(ckv_list, axis=1)
    new_mla_krope = jnp.stack(krope_list, axis=1)
    new_swa_k = jnp.stack(k_list, axis=1)
    new_swa_v = jnp.stack(v_list, axis=1)
    return (y_prompt, y_sample, new_mla_ckv, new_mla_krope, new_swa_k, new_swa_v)
```

```python
import functools
import math

import jax
import jax.numpy as jnp
from jax import lax
from jax.experimental import pallas as pl
from jax.experimental.pallas import tpu as pltpu

D_MODEL = 1024
BATCH, SEQ = 16, 256
DEPTH = 2
DEC_BATCH, DEC_SEQ = 2, 1024
PAST_LEN = 512
GRID_W = 64
EPS = 1e-6
ROPE_THETA = 10000.0
A_HEADS, A_NOPE, A_ROPE, A_V = 8, 64, 32, 64
A_QK = A_NOPE + A_ROPE
Q_LORA, KV_LORA = 384, 256
B_GROUPS, B_WIDTH, B_CHUNK = 8, 512, 128
C_HEADS, C_KV_HEADS, C_HEAD_DIM, WINDOW = 8, 2, 64, 128
C_GROUP = C_HEADS // C_KV_HEADS
P_HEADS, N_KEYS, P_QDIM, P_TOPK = 8, 128, 256, 16
N_EXPERTS = N_KEYS * N_KEYS
N_BRANCH = 3

LANE = 128
VMEM_LIMIT = 56 * 1024 * 1024

N_CTX = BATCH * SEQ
N_LAT = DEC_BATCH * DEC_SEQ
N_TOK = N_CTX + N_LAT
TM = 256
N_TILES = N_TOK // TM
CTX_TILES = N_CTX // TM
LAT_TILES_PER_B = DEC_SEQ // TM
QT = 128
PEER_TS = 256
PEER_TT = 512
PEER_EC = 1024
NEG = -1e30

_IN_COLS = (("cq", Q_LORA), ("ckv", KV_LORA), ("krope", LANE), ("u", B_WIDTH), ("v", B_WIDTH),
            ("qc", C_HEADS * LANE), ("kc", C_KV_HEADS * LANE), ("vc", C_KV_HEADS * LANE),
            ("gates", N_BRANCH * D_MODEL))
_IN_OFF = {}
_o = 0
for _n, _w in _IN_COLS:
    _IN_OFF[_n] = (_o, _o + _w)
    _o += _w
IN_PAD = _o

f32 = jnp.float32
bf16 = jnp.bfloat16


def _cparams(sem):
    return pltpu.CompilerParams(dimension_semantics=sem, vmem_limit_bytes=VMEM_LIMIT)


def _dot(a, b):
    return jnp.dot(a, b, preferred_element_type=f32)


def _dot_nt(a, b):
    return lax.dot_general(a, b, (((1,), (1,)), ((), ())), preferred_element_type=f32)


def _rms(x, n):
    return x * lax.rsqrt(jnp.sum(x * x, axis=-1, keepdims=True) * (1.0 / n) + EPS)


def _gelu(x):
    return 0.5 * x * (1.0 + jnp.tanh(math.sqrt(2.0 / math.pi) * (x + 0.044715 * (x * x * x))))


def _sigmoid(x):
    return 1.0 / (1.0 + jnp.exp(-x))


def _cond_of_tile(i, tiles_per_lat_batch, ctx_tiles):
    return jnp.where(i < ctx_tiles, 0, 1 + (i - ctx_tiles) // tiles_per_lat_batch)


MOD_BN = 1536


def _mod_kernel(c_ref, w_ref, b_ref, o_ref):
    c = c_ref[...]
    s = (c * _sigmoid(c)).astype(bf16)
    o_ref[...] = _dot(s, w_ref[...].astype(bf16)) + b_ref[...]


def _modulation(conds8, w_mod, b_mod):
    n = 6 * D_MODEL
    return pl.pallas_call(
        _mod_kernel,
        grid=(n // MOD_BN,),
        in_specs=[pl.BlockSpec((8, D_MODEL), lambda j: (0, 0)),
                  pl.BlockSpec((D_MODEL, MOD_BN), lambda j: (0, j)),
                  pl.BlockSpec((1, MOD_BN), lambda j: (0, j))],
        out_specs=pl.BlockSpec((8, MOD_BN), lambda j: (0, j)),
        out_shape=jax.ShapeDtypeStruct((8, n), f32),
        compiler_params=_cparams(("arbitrary",)),
        name="adaln_mod",
    )(conds8, w_mod, b_mod.reshape(1, n))


def _rope(x, c, s_next, s_prev, shift):
    return (x * c + pltpu.roll(x, LANE - shift, axis=1) * s_next + pltpu.roll(x, shift, axis=1) * s_prev)


def _mla_kv(ckvn_bf, kr, wk_ref, wv_ref, gk, rope, ka_ref, va_ref):
    kk = _dot(ckvn_bf, wk_ref[...])
    for h in range(A_HEADS):
        blk = kk[:, h * LANE:(h + 1) * LANE] + kr
        blk = _rms(blk, A_QK) * gk
        if rope is not None:
            blk = _rope(blk, rope[0], rope[1], rope[2], A_ROPE // 2)
        ka_ref[:, h * LANE:(h + 1) * LANE] = blk.astype(bf16)
    va_ref[...] = _dot(ckvn_bf, wv_ref[...]).astype(bf16)


def _in_kernel(x_ref, mod_ref, gattn_ref, win_ref, gql_ref, wuq_ref, gkvl_ref, wk_ref, wv_ref,
               gqa_ref, gka_ref, gvb_ref, ws_ref, bs_ref, gqc_ref, gkc_ref,
               ca_ref, sna_ref, spa_ref, cc_ref, snc_ref, spc_ref,
               qa_ref, ka_ref, va_ref, ckvn_ref, kr_ref, ob_ref, qc_ref, kc_ref, vc_ref, gates_ref):
    mod = mod_ref[0]
    sh1, sc1 = mod[0:1], mod[1:2]
    h = (_rms(x_ref[...], D_MODEL) * gattn_ref[...] * (1.0 + sc1) + sh1).astype(bf16)

    def proj(name):
        lo, hi = _IN_OFF[name]
        return _dot(h, win_ref[:, lo:hi])

    rope_a = (ca_ref[...], sna_ref[...], spa_ref[...])
    rope_c = (cc_ref[...], snc_ref[...], spc_ref[...])

    cqn = (_rms(proj("cq"), Q_LORA) * gql_ref[...]).astype(bf16)
    qraw = _dot(cqn, wuq_ref[...])
    gqa = gqa_ref[...]
    for hd in range(A_HEADS):
        blk = _rms(qraw[:, hd * LANE:(hd + 1) * LANE], A_QK) * gqa
        blk = _rope(blk, *rope_a, A_ROPE // 2) * (A_QK ** -0.5)
        qa_ref[:, hd * LANE:(hd + 1) * LANE] = blk.astype(bf16)

    ckvn = _rms(proj("ckv"), KV_LORA) * gkvl_ref[...]
    ckvn_ref[...] = ckvn
    kr = proj("krope")
    kr_ref[...] = kr
    _mla_kv(ckvn.astype(bf16), kr, wk_ref, wv_ref, gka_ref[...], rope_a, ka_ref, va_ref)

    u = _gelu(proj("u"))
    vn = (_rms(_gelu(proj("v")), B_WIDTH) * gvb_ref[...])
    lane_group = lax.broadcasted_iota(jnp.int32, (B_CHUNK, B_WIDTH), 1) // (B_WIDTH // B_GROUPS)
    for ch in range(TM // B_CHUNK):
        vch = vn[ch * B_CHUNK:(ch + 1) * B_CHUNK]
        vm = bs_ref[...]
        for g in range(B_GROUPS):
            vg = jnp.where(lane_group == g, vch, 0.0).astype(bf16)
            vm = vm + _dot(ws_ref[g], vg)
        ob_ref[ch * B_CHUNK:(ch + 1) * B_CHUNK, :] = (u[ch * B_CHUNK:(ch + 1) * B_CHUNK] * vm).astype(bf16)

    qcr = proj("qc")
    gqc = gqc_ref[...]
    for hd in range(C_HEADS):
        blk = _rms(qcr[:, hd * LANE:(hd + 1) * LANE], C_HEAD_DIM) * gqc
        blk = _rope(blk, *rope_c, C_HEAD_DIM // 2) * (C_HEAD_DIM ** -0.5)
        qc_ref[:, hd * LANE:(hd + 1) * LANE] = blk.astype(bf16)
    kcr = proj("kc")
    gkc = gkc_ref[...]
    for hd in range(C_KV_HEADS):
        blk = _rms(kcr[:, hd * LANE:(hd + 1) * LANE], C_HEAD_DIM) * gkc
        kc_ref[:, hd * LANE:(hd + 1) * LANE] = _rope(blk, *rope_c, C_HEAD_DIM // 2)
    vc_ref[...] = proj("vc")

    gates_ref[...] = _sigmoid(proj("gates")).astype(bf16)


def _const_spec(shape):
    nd = len(shape)
    return pl.BlockSpec(shape, lambda i, _nd=nd: (0,) * _nd, pipeline_mode=pl.Buffered(1))


def _input_stage(x, mods, lw, rope_tabs):
    cond = functools.partial(_cond_of_tile, tiles_per_lat_batch=LAT_TILES_PER_B, ctx_tiles=CTX_TILES)
    rope_blk = lambda i: (jnp.where(i < CTX_TILES, LAT_TILES_PER_B, (i - CTX_TILES) % LAT_TILES_PER_B), 0)
    tok = lambda w: pl.BlockSpec((TM, w), lambda i: (i, 0))
    consts = [lw["g_attn"], lw["w_in"], lw["g_q_lora"], lw["w_uq"], lw["g_kv_lora"], lw["w_uk"], lw["w_uv"],
              lw["g_qk_a_q"], lw["g_qk_a_k"], lw["g_v_b"], lw["w_s"], lw["b_s"], lw["g_qk_c_q"], lw["g_qk_c_k"]]
    in_specs = ([tok(D_MODEL), pl.BlockSpec((1, 6, D_MODEL), lambda i: (cond(i), 0, 0))]
                + [_const_spec(c.shape) for c in consts]
                + [pl.BlockSpec((TM, LANE), rope_blk)] * 6)
    widths = (("qa", A_HEADS * LANE, bf16), ("ka", A_HEADS * LANE, bf16), ("va", A_HEADS * LANE, bf16),
              ("ckvn", KV_LORA, f32), ("kr", LANE, f32), ("ob", B_WIDTH, bf16), ("qc", C_HEADS * LANE, bf16),
              ("kc", C_KV_HEADS * LANE, f32), ("vc", C_KV_HEADS * LANE, f32), ("gates", N_BRANCH * D_MODEL, bf16))
    outs = pl.pallas_call(
        _in_kernel,
        grid=(N_TILES,),
        in_specs=in_specs,
        out_specs=[tok(w) for _, w, _ in widths],
        out_shape=[jax.ShapeDtypeStruct((N_TOK, w), dt) for _, w, dt in widths],
        compiler_params=_cparams(("parallel",)),
        name="input_stage",
    )(x, mods, *consts, *rope_tabs)
    return {n: o for (n, _, _), o in zip(widths, outs)}


def _ctxkv_kernel(ckv_ref, kr_ref, wk_ref, wv_ref, gka_ref, ka_ref, va_ref):
    _mla_kv(ckv_ref[...].astype(bf16), kr_ref[...], wk_ref, wv_ref, gka_ref[...], None, ka_ref, va_ref)


def _ctx_kv(ckv, kr_pad, lw):
    n = ckv.shape[0]
    tok = lambda w: pl.BlockSpec((TM, w), lambda i: (i, 0))
    return pl.pallas_call(
        _ctxkv_kernel,
        grid=(n // TM,),
        in_specs=[tok(KV_LORA), tok(LANE), _const_spec(lw["w_uk"].shape), _const_spec(lw["w_uv"].shape),
                  _const_spec(lw["g_qk_a_k"].shape)],
        out_specs=[tok(A_HEADS * LANE), tok(A_HEADS * LANE)],
        out_shape=[jax.ShapeDtypeStruct((n, A_HEADS * LANE), bf16)] * 2,
        compiler_params=_cparams(("parallel",)),
        name="ctx_kv",
    )(ckv, kr_pad, lw["w_uk"], lw["w_uv"], lw["g_qk_a_k"])


def _softmax_pv(scores, values, sink=None):
    m = functools.reduce(jnp.maximum, [jnp.max(s, axis=-1, keepdims=True) for s in scores])
    if sink is not None:
        m = jnp.maximum(m, sink)
    es = [jnp.exp(s - m) for s in scores]
    den = functools.reduce(jnp.add, [jnp.sum(e, axis=-1, keepdims=True) for e in es])
    if sink is not None:
        den = den + jnp.exp(sink - m)
    o = functools.reduce(jnp.add, [_dot(e.astype(bf16), v) for e, v in zip(es, values)])
    return o * (1.0 / den)


def _attn_ctx_kernel(sink_ref, qa_ref, ka_ref, va_ref, qc_ref, kc_ref, vc_ref, oa_ref, oc_ref):
    for h in range(A_HEADS):
        sl = slice(h * LANE, (h + 1) * LANE)
        s = _dot_nt(qa_ref[:, sl], ka_ref[:, sl])
        oa_ref[:, sl] = _softmax_pv([s], [va_ref[:, sl]]).astype(bf16)
    for j in range(C_KV_HEADS):
        kj = kc_ref[:, j * LANE:(j + 1) * LANE].astype(bf16)
        vj = vc_ref[:, j * LANE:(j + 1) * LANE].astype(bf16)
        for g in range(C_GROUP):
            hq = j * C_GROUP + g
            sl = slice(hq * LANE, (hq + 1) * LANE)
            s = _dot_nt(qc_ref[:, sl], kj)
            oc_ref[:, sl] = _softmax_pv([s], [vj], sink=sink_ref[hq]).astype(bf16)


def _attention_ctx(st, sink):
    tok = lambda w: pl.BlockSpec((SEQ, w), lambda b: (b, 0))
    return pl.pallas_call(
        _attn_ctx_kernel,
        grid=(BATCH,),
        in_specs=[pl.BlockSpec(memory_space=pltpu.SMEM), tok(A_HEADS * LANE), tok(A_HEADS * LANE),
                  tok(A_HEADS * LANE), tok(C_HEADS * LANE), tok(C_KV_HEADS * LANE), tok(C_KV_HEADS * LANE)],
        out_specs=[tok(A_HEADS * LANE), tok(C_HEADS * LANE)],
        out_shape=[jax.ShapeDtypeStruct((N_CTX, A_HEADS * LANE), bf16),
                   jax.ShapeDtypeStruct((N_CTX, C_HEADS * LANE), bf16)],
        compiler_params=_cparams(("parallel",)),
        name="attn_ctx",
    )(sink, st["qa"], st["ka"], st["va"], st["qc"], st["kc"], st["vc"])


LOCAL_KEYS = 3 * QT


def _attn_lat_kernel(sink_ref, qa_ref, ka_ref, va_ref, kax_ref, vax_ref, qc_ref, kc_ref, vc_ref, kcx_ref, vcx_ref,
                     oa_ref, oc_ref):
    qi = pl.program_id(1)
    for h in range(A_HEADS):
        sl = slice(h * LANE, (h + 1) * LANE)
        q = qa_ref[:, sl]
        s_lat = _dot_nt(q, ka_ref[:, sl])
        s_ctx = _dot_nt(q, kax_ref[:, sl])
        oa_ref[:, sl] = _softmax_pv([s_lat, s_ctx], [va_ref[:, sl], vax_ref[:, sl]]).astype(bf16)
    start = pl.multiple_of(jnp.clip(qi - 1, 0, DEC_SEQ // QT - 3) * QT, QT)
    qpos = qi * QT + lax.broadcasted_iota(jnp.int32, (QT, LOCAL_KEYS), 0)
    kpos = start + lax.broadcasted_iota(jnp.int32, (QT, LOCAL_KEYS), 1)
    valid = jnp.abs(qpos - kpos) <= WINDOW
    for j in range(C_KV_HEADS):
        jl = slice(j * LANE, (j + 1) * LANE)
        k_loc = kc_ref[pl.ds(start, LOCAL_KEYS), jl].astype(bf16)
        v_loc = vc_ref[pl.ds(start, LOCAL_KEYS), jl].astype(bf16)
        k_ctx = kcx_ref[:, jl]
        v_ctx = vcx_ref[:, jl]
        for g in range(C_GROUP):
            hq = j * C_GROUP + g
            sl = slice(hq * LANE, (hq + 1) * LANE)
            q = qc_ref[:, sl]
            s_loc = jnp.where(valid, _dot_nt(q, k_loc), NEG)
            s_ctx = _dot_nt(q, k_ctx)
            oc_ref[:, sl] = _softmax_pv([s_loc, s_ctx], [v_loc, v_ctx], sink=sink_ref[hq]).astype(bf16)


def _attention_lat(st, kax, vax, kcx, vcx, sink):
    nq = DEC_SEQ // QT
    lat0 = N_CTX // QT
    lat0_b = N_CTX // DEC_SEQ
    qtile = lambda w: pl.BlockSpec((QT, w), lambda b, q: (lat0 + b * nq + q, 0))
    whole = lambda w: pl.BlockSpec((DEC_SEQ, w), lambda b, q: (lat0_b + b, 0))
    ctx = lambda w: pl.BlockSpec((PAST_LEN, w), lambda b, q: (b, 0))
    out = lambda w: pl.BlockSpec((QT, w), lambda b, q: (b * nq + q, 0))
    return pl.pallas_call(
        _attn_lat_kernel,
        grid=(DEC_BATCH, nq),
        in_specs=[pl.BlockSpec(memory_space=pltpu.SMEM),
                  qtile(A_HEADS * LANE), whole(A_HEADS * LANE), whole(A_HEADS * LANE),
                  ctx(A_HEADS * LANE), ctx(A_HEADS * LANE),
                  qtile(C_HEADS * LANE), whole(C_KV_HEADS * LANE), whole(C_KV_HEADS * LANE),
                  ctx(C_KV_HEADS * LANE), ctx(C_KV_HEADS * LANE)],
        out_specs=[out(A_HEADS * LANE), out(C_HEADS * LANE)],
        out_shape=[jax.ShapeDtypeStruct((N_LAT, A_HEADS * LANE), bf16),
                   jax.ShapeDtypeStruct((N_LAT, C_HEADS * LANE), bf16)],
        compiler_params=_cparams(("parallel", "arbitrary")),
        name="attn_lat",
    )(sink, st["qa"], st["ka"], st["va"], kax, vax, st["qc"], st["kc"], st["vc"], kcx, vcx)


def _out_kernel(x_ref, mod_ref, oa_ref, ob_ref, oc_ref, gates_ref, woa_ref, wob_ref, woc_ref, wout_ref, gffn_ref,
                x1_ref, h2_ref):
    mod = mod_ref[0]
    g1, sh2, sc2 = mod[2:3], mod[3:4], mod[4:5]
    merged = (gates_ref[:, 0:D_MODEL].astype(f32) * _dot(oa_ref[...], woa_ref[...])
              + gates_ref[:, D_MODEL:2 * D_MODEL].astype(f32) * _dot(ob_ref[...], wob_ref[...])
              + gates_ref[:, 2 * D_MODEL:3 * D_MODEL].astype(f32) * _dot(oc_ref[...], woc_ref[...]))
    x1 = x_ref[...] + g1 * _dot(merged.astype(bf16), wout_ref[...])
    x1_ref[...] = x1
    h2_ref[...] = (_rms(x1, D_MODEL) * gffn_ref[...] * (1.0 + sc2) + sh2).astype(bf16)


def _output_stage(x, mods, oa, ob, oc, gates, lw):
    cond = functools.partial(_cond_of_tile, tiles_per_lat_batch=LAT_TILES_PER_B, ctx_tiles=CTX_TILES)
    tok = lambda w: pl.BlockSpec((TM, w), lambda i: (i, 0))
    consts = [lw["w_o_a"], lw["w_o_b"], lw["w_o_c"], lw["w_out"], lw["g_ffn"]]
    return pl.pallas_call(
        _out_kernel,
        grid=(N_TILES,),
        in_specs=[tok(D_MODEL), pl.BlockSpec((1, 6, D_MODEL), lambda i: (cond(i), 0, 0)),
                  tok(A_HEADS * LANE), tok(B_WIDTH), tok(C_HEADS * LANE), tok(N_BRANCH * D_MODEL)]
                 + [_const_spec(c.shape) for c in consts],
        out_specs=[tok(D_MODEL), tok(D_MODEL)],
        out_shape=[jax.ShapeDtypeStruct((N_TOK, D_MODEL), f32), jax.ShapeDtypeStruct((N_TOK, D_MODEL), bf16)],
        compiler_params=_cparams(("parallel",)),
        name="output_stage",
    )(x, mods, oa, ob, oc, gates, *consts)


TOP_ROWS = 24
N_TOP = P_TOPK + 1


def _peer_sel_kernel(h2_ref, wpq_ref, kh_ref, kl_ref, c0_ref, a0_ref, s1_ref, b1_ref, h2t_ref,
                     q_scr, st_scr, top_scr):
    ts = h2_ref.shape[0]
    h2 = h2_ref[...]
    q = _dot(h2, wpq_ref[...])
    for hp in range(2 * P_HEADS):
        q_scr[hp] = q[:, hp * LANE:(hp + 1) * LANE]
    h2t_ref[...] = h2.astype(f32).T.astype(bf16)
    kiota = lax.broadcasted_iota(jnp.int32, (N_KEYS, ts), 0).astype(f32)

    def half_scores(hp, carry):
        q = q_scr[hp]
        q_hi = q.astype(bf16)
        q_lo = (q - q_hi.astype(f32)).astype(bf16)
        kh = kh_ref[hp]
        st = _dot_nt(kh, q_hi) + (_dot_nt(kh, q_lo) + _dot_nt(kl_ref[hp], q_hi))
        st_scr[hp] = st
        top_scr[hp] = jnp.full((TOP_ROWS, ts), NEG, f32)
        v = st
        for p in range(N_TOP):
            m = jnp.max(v, axis=0, keepdims=True)
            idx = jnp.min(jnp.where(v == m, kiota, float(N_KEYS)), axis=0, keepdims=True)
            top_scr[hp, p:p + 1, :] = m
            v = jnp.where(kiota == idx, NEG, v)
        return carry

    lax.fori_loop(0, 2 * P_HEADS, half_scores, 0)

    low4 = lax.broadcasted_iota(jnp.int32, (8, ts), 0) >= 4

    def head_gates(h, carry):
        a = top_scr[2 * h]
        b = top_scr[2 * h + 1]
        a0, a1, a2 = a[0:1], a[1:2], a[2:3]
        b0, b1, b2, b3 = b[0:1], b[1:2], b[2:3], b[3:4]
        blocks = [a[0:8] + b0, a[8:16] + b0, a[16:24] + b0, a[0:8] + b1, a[0:8] + b2, a[0:8] + b3,
                  jnp.where(low4, b[0:8] + a0, NEG), b[8:16] + a0, b[16:24] + a0,
                  jnp.where(low4, b[0:8] + a1, NEG), jnp.where(low4, b[0:8] + a2, NEG)]
        orig = jnp.concatenate(blocks, axis=0)
        cand = orig
        taken = jnp.zeros((1, ts), f32)
        thr16 = jnp.full((1, ts), NEG, f32)
        thr17 = jnp.full((1, ts), NEG, f32)
        for _ in range(N_TOP):
            m = jnp.max(cand, axis=0, keepdims=True)
            eq = cand == m
            new = taken + jnp.sum(jnp.where(eq, 1.0, 0.0), axis=0, keepdims=True)
            thr16 = jnp.where(taken < P_TOPK, jnp.where(new >= P_TOPK, m, thr16), thr16)
            thr17 = jnp.where(taken < N_TOP, jnp.where(new >= N_TOP, m, thr17), thr17)
            taken = new
            cand = jnp.where(eq, NEG, cand)
        m0 = a0 + b0
        z = jnp.sum(jnp.where(orig >= thr16, jnp.exp(orig - m0), 0.0), axis=0, keepdims=True)
        thr = 0.5 * (thr16 + thr17)
        s0 = st_scr[2 * h]
        s1 = st_scr[2 * h + 1]
        c0_ref[h] = thr - s0
        a0_ref[h] = jnp.exp(s0 - a0)
        s1_ref[h] = s1
        b1_ref[h] = jnp.exp(s1 - b0) * (1.0 / z)
        return carry

    lax.fori_loop(0, P_HEADS, head_gates, 0)


def _peer_select(h2, lw):
    ts = PEER_TS
    fac = pl.BlockSpec((P_HEADS, N_KEYS, ts), lambda i: (0, 0, i))
    fac_shape = jax.ShapeDtypeStruct((P_HEADS, N_KEYS, N_TOK), f32)
    return pl.pallas_call(
        _peer_sel_kernel,
        grid=(N_TOK // ts,),
        in_specs=[pl.BlockSpec((ts, D_MODEL), lambda i: (i, 0)), _const_spec(lw["w_pq"].shape),
                  _const_spec(lw["keys_hi"].shape), _const_spec(lw["keys_lo"].shape)],
        out_specs=[fac, fac, fac, fac, pl.BlockSpec((D_MODEL, ts), lambda i: (0, i))],
        out_shape=[fac_shape] * 4 + [jax.ShapeDtypeStruct((D_MODEL, N_TOK), bf16)],
        scratch_shapes=[pltpu.VMEM((2 * P_HEADS, ts, LANE), f32),
                        pltpu.VMEM((2 * P_HEADS, N_KEYS, ts), f32),
                        pltpu.VMEM((2 * P_HEADS, TOP_ROWS, ts), f32)],
        compiler_params=_cparams(("parallel",)),
        name="peer_select",
    )(h2, lw["w_pq"], lw["keys_hi"], lw["keys_lo"])


ROWS = 16


def _peer_kernel(h2t_ref, u_ref, vt_ref, c0_ref, a0_ref, s1_ref, b1_ref, x1_ref, mod_ref, y_ref,
                 acc_scr, act_scr, wa_scr):
    c = pl.program_id(1)

    @pl.when(c == 0)
    def _():
        acc_scr[...] = jnp.zeros_like(acc_scr)

    act_scr[...] = _dot(u_ref[...], h2t_ref[...])

    n_i = PEER_EC // N_KEYS
    for r in range(N_KEYS // ROWS):
        rs = slice(r * ROWS, (r + 1) * ROWS)

        def body(i, carry, rs=rs, r=r):
            w = jnp.zeros((ROWS, PEER_TT), f32)
            for h in range(P_HEADS):
                c0 = c0_ref[h, pl.ds(i, 1), :]
                a0 = a0_ref[h, pl.ds(i, 1), :]
                w = w + jnp.where(s1_ref[h, rs, :] >= c0, a0 * b1_ref[h, rs, :], 0.0)
            row = pl.multiple_of(i * N_KEYS + r * ROWS, ROWS)
            act = act_scr[pl.ds(row, ROWS), :]
            wa_scr[pl.ds(row, ROWS), :] = (w * _gelu(act)).astype(bf16)
            return carry

        lax.fori_loop(0, n_i, body, 0)

    acc_scr[...] += _dot(vt_ref[...], wa_scr[...])

    @pl.when(c == pl.num_programs(1) - 1)
    def _():
        g2 = mod_ref[0][5:6]
        y_ref[...] = x1_ref[...] + g2 * acc_scr[...].T


def _peer_dense(h2t, fac, x1, mods, lw):
    tt, ec = PEER_TT, PEER_EC
    c0, a0, s1, b1 = fac
    tiles_per_b = DEC_SEQ // tt
    cond = functools.partial(_cond_of_tile, tiles_per_lat_batch=tiles_per_b, ctx_tiles=N_CTX // tt)
    row_fac = pl.BlockSpec((P_HEADS, ec // N_KEYS, tt), lambda t, c: (0, c, t))
    all_fac = pl.BlockSpec((P_HEADS, N_KEYS, tt), lambda t, c: (0, 0, t))
    return pl.pallas_call(
        _peer_kernel,
        grid=(N_TOK // tt, N_EXPERTS // ec),
        in_specs=[pl.BlockSpec((D_MODEL, tt), lambda t, c: (0, t)),
                  pl.BlockSpec((ec, D_MODEL), lambda t, c: (c, 0)),
                  pl.BlockSpec((D_MODEL, ec), lambda t, c: (0, c)),
                  row_fac, row_fac, all_fac, all_fac,
                  pl.BlockSpec((tt, D_MODEL), lambda t, c: (t, 0)),
                  pl.BlockSpec((1, 6, D_MODEL), lambda t, c: (cond(t), 0, 0))],
        out_specs=pl.BlockSpec((tt, D_MODEL), lambda t, c: (t, 0)),
        out_shape=jax.ShapeDtypeStruct((N_TOK, D_MODEL), f32),
        scratch_shapes=[pltpu.VMEM((D_MODEL, tt), f32), pltpu.VMEM((ec, tt), f32), pltpu.VMEM((ec, tt), bf16)],
        compiler_params=_cparams(("parallel", "arbitrary")),
        name="peer_dense",
    )(h2t, lw["peer_u"], lw["peer_vt"], c0, a0, s1, b1, x1, mods)


def _pad_heads(w, n_heads, d):
    lead = w.shape[:-1]
    w = w.reshape(lead + (n_heads, d))
    w = jnp.pad(w, [(0, 0)] * len(lead) + [(0, 0), (0, LANE - d)])
    return w.reshape(lead + (n_heads * LANE,))


def _pad_head_rows(w, n_heads, d):
    return _pad_heads(w.T, n_heads, d).T


def _lane_block(v, lo):
    return jnp.pad(v, [(0, 0)] * (v.ndim - 1) + [(lo, LANE - lo - v.shape[-1])])


def _layer_params(l, p):
    w_in = p["w_in"][l]
    offs, o = [], 0
    for wdt in (Q_LORA, KV_LORA, A_ROPE, 2 * B_WIDTH, C_HEADS * C_HEAD_DIM, C_KV_HEADS * C_HEAD_DIM,
                C_KV_HEADS * C_HEAD_DIM, N_BRANCH * D_MODEL):
        offs.append((o, o + wdt))
        o += wdt
    cq, ckv, krope, zb, qc, kc, vc, gates = [w_in[:, a:b] for a, b in offs]
    w_in_p = jnp.concatenate([cq, ckv, _lane_block(krope, A_NOPE), zb, _pad_heads(qc, C_HEADS, C_HEAD_DIM),
                              _pad_heads(kc, C_KV_HEADS, C_HEAD_DIM), _pad_heads(vc, C_KV_HEADS, C_HEAD_DIM),
                              gates], axis=1).astype(bf16)
    w_ukv = p["w_ukv"][l].reshape(KV_LORA, A_HEADS, A_NOPE + A_V)
    keys = p["peer_keys"][l].reshape(2 * P_HEADS, N_KEYS, P_QDIM // 2)
    keys_hi = keys.astype(bf16)
    row = lambda v: v.reshape(1, -1)
    return {
        "g_attn": row(p["g_attn_norm"][l]),
        "w_in": w_in_p,
        "g_q_lora": row(p["g_q_lora"][l]),
        "w_uq": _pad_heads(p["w_uq"][l], A_HEADS, A_QK).astype(bf16),
        "g_kv_lora": row(p["g_kv_lora"][l]),
        "w_uk": _pad_heads(w_ukv[..., :A_NOPE].reshape(KV_LORA, -1), A_HEADS, A_NOPE).astype(bf16),
        "w_uv": _pad_heads(w_ukv[..., A_NOPE:].reshape(KV_LORA, -1), A_HEADS, A_V).astype(bf16),
        "g_qk_a_q": _lane_block(row(p["g_qk_a_q"][l]), 0),
        "g_qk_a_k": _lane_block(row(p["g_qk_a_k"][l]), 0),
        "g_v_b": row(p["g_v_b"][l]),
        "w_s": p["w_s_b"][l].astype(bf16),
        "b_s": jnp.repeat(p["b_s_b"][l].T, B_WIDTH // B_GROUPS, axis=1),
        "g_qk_c_q": _lane_block(row(p["g_qk_c_q"][l]), 0),
        "g_qk_c_k": _lane_block(row(p["g_qk_c_k"][l]), 0),
        "sink": p["sink_c"][l],
        "w_o_a": _pad_head_rows(p["w_o_a"][l], A_HEADS, A_V).astype(bf16),
        "w_o_b": p["w_o_b"][l].astype(bf16),
        "w_o_c": _pad_head_rows(p["w_o_c"][l], C_HEADS, C_HEAD_DIM).astype(bf16),
        "w_out": p["w_out"][l].astype(bf16),
        "g_ffn": row(p["g_ffn_norm"][l]),
        "w_pq": p["w_pq"][l].astype(bf16),
        "keys_hi": keys_hi,
        "keys_lo": (keys - keys_hi.astype(f32)).astype(bf16),
        "peer_u": p["peer_u"][l].astype(bf16),
        "peer_vt": p["peer_v"][l].T.astype(bf16),
    }


def _rope_tables():
    pos = jnp.arange(DEC_SEQ)
    row = (pos // GRID_W).astype(f32)
    col = (pos % GRID_W).astype(f32)

    def tables(dim, lo):
        quarter = dim // 4
        inv = ROPE_THETA ** (-jnp.arange(quarter, dtype=f32) / quarter)
        ang = jnp.concatenate([row[:, None] * inv, col[:, None] * inv], axis=-1)
        cos, sin = jnp.cos(ang), jnp.sin(ang)
        half = dim // 2
        c = jnp.ones((DEC_SEQ, LANE), f32).at[:, lo:lo + half].set(cos).at[:, lo + half:lo + dim].set(cos)
        s_next = jnp.zeros((DEC_SEQ, LANE), f32).at[:, lo:lo + half].set(-sin)
        s_prev = jnp.zeros((DEC_SEQ, LANE), f32).at[:, lo + half:lo + dim].set(sin)
        ident = [jnp.ones((TM, LANE), f32), jnp.zeros((TM, LANE), f32), jnp.zeros((TM, LANE), f32)]
        return [jnp.concatenate([t, i], axis=0) for t, i in zip((c, s_next, s_prev), ident)]

    return tables(A_ROPE, A_NOPE) + tables(C_HEAD_DIM, 0)


def _layer(x, mods, lw, rope_tabs, cache):
    st = _input_stage(x, mods, lw, rope_tabs)
    ckv_c, kr_c, kcx, vcx = cache
    kax, vax = _ctx_kv(ckv_c, kr_c, lw)
    oa_c, oc_c = _attention_ctx(st, lw["sink"])
    oa_l, oc_l = _attention_lat(st, kax, vax, kcx, vcx, lw["sink"])
    oa = jnp.concatenate([oa_c, oa_l], axis=0)
    oc = jnp.concatenate([oc_c, oc_l], axis=0)
    x1, h2 = _output_stage(x, mods, oa, st["ob"], oc, st["gates"], lw)
    *fac, h2t = _peer_select(h2, lw)
    y = _peer_dense(h2t, fac, x1, mods, lw)
    return y, st


def kernel(x_prompt, x_sample, c, cache_mla_ckv, cache_mla_krope, cache_swa_k, cache_swa_v, c_ctx, w_mod, b_mod, g_attn_norm, w_in, g_q_lora, w_uq, g_kv_lora, w_ukv, g_qk_a_q, g_qk_a_k, w_o_a, g_v_b, w_s_b, b_s_b, w_o_b, g_qk_c_q, g_qk_c_k, sink_c, w_o_c, w_out, g_ffn_norm, w_pq, peer_keys, peer_u, peer_v):
    p = dict(g_attn_norm=g_attn_norm, w_in=w_in, g_q_lora=g_q_lora, w_uq=w_uq, g_kv_lora=g_kv_lora, w_ukv=w_ukv,
             g_qk_a_q=g_qk_a_q, g_qk_a_k=g_qk_a_k, w_o_a=w_o_a, g_v_b=g_v_b, w_s_b=w_s_b, b_s_b=b_s_b, w_o_b=w_o_b,
             g_qk_c_q=g_qk_c_q, g_qk_c_k=g_qk_c_k, sink_c=sink_c, w_o_c=w_o_c, w_out=w_out, g_ffn_norm=g_ffn_norm,
             w_pq=w_pq, peer_keys=peer_keys, peer_u=peer_u, peer_v=peer_v)
    x = jnp.concatenate([x_prompt.reshape(N_CTX, D_MODEL), x_sample.reshape(N_LAT, D_MODEL)], axis=0)
    conds8 = jnp.concatenate([c_ctx[None, :], c, jnp.zeros((8 - 1 - DEC_BATCH, D_MODEL), f32)], axis=0)
    rope_tabs = _rope_tables()
    n_cache = DEC_BATCH * PAST_LEN
    new_ckv, new_kr, new_k, new_v = [], [], [], []
    for l in range(DEPTH):
        lw = _layer_params(l, p)
        mods = _modulation(conds8, w_mod[l], b_mod[l]).reshape(8, 6, D_MODEL)
        cache = (cache_mla_ckv[:, l].reshape(n_cache, KV_LORA),
                 _lane_block(cache_mla_krope[:, l].reshape(n_cache, A_ROPE), A_NOPE),
                 _pad_heads(cache_swa_k[:, l].reshape(n_cache, -1), C_KV_HEADS, C_HEAD_DIM).astype(bf16),
                 _pad_heads(cache_swa_v[:, l].reshape(n_cache, -1), C_KV_HEADS, C_HEAD_DIM).astype(bf16))
        x, st = _layer(x, mods, lw, rope_tabs, cache)
        new_ckv.append(st["ckvn"][:N_CTX].reshape(BATCH, SEQ, KV_LORA))
        new_kr.append(st["kr"][:N_CTX, A_NOPE:A_QK].reshape(BATCH, SEQ, A_ROPE))
        unpad = lambda t: t[:N_CTX].reshape(BATCH, SEQ, C_KV_HEADS, LANE)[..., :C_HEAD_DIM]
        new_k.append(unpad(st["kc"]))
        new_v.append(unpad(st["vc"]))
    y_prompt = x[:N_CTX].reshape(BATCH, SEQ, D_MODEL)
    y_sample = x[N_CTX:].reshape(DEC_BATCH, DEC_SEQ, D_MODEL)
    stack = lambda ts: jnp.stack(ts, axis=1)
    return (y_prompt, y_sample, stack(new_ckv), stack(new_kr), stack(new_k), stack(new_v))
```
